```python
import math
import numpy as np
import jax
import jax.numpy as jnp
from jax import lax

D_MODEL = 2048
BATCH = 8
SEQ = 2048
DEPTH = 1

CHUNK = 64
BLOCK_Q = 128
FOX_HEADS = 8
FOX_HEAD_DIM = 128
FOX_WIDTH = FOX_HEADS * FOX_HEAD_DIM
RET_HEADS = 8
RET_QK_DIM = 128
RET_V_DIM = 128
RET_QK_WIDTH = RET_HEADS * RET_QK_DIM
RET_V_WIDTH = RET_HEADS * RET_V_DIM
N_BRANCHES = 2
MEM_TOKENS = 256
MEM_HEADS = 4
MEM_HEAD_DIM = 128
MEM_WIDTH = MEM_HEADS * MEM_HEAD_DIM
PEER_HEADS = 8
PEER_NKEYS = 128
PEER_QUERY_DIM = 256
PEER_TOPK = 16
N_EXPERTS = PEER_NKEYS * PEER_NKEYS
PEER_TOKEN_BLOCK = 128
ROPE_BASE = 10000.0
EPS = 1e-6
IN_SPLIT_SIZES = (FOX_WIDTH, FOX_WIDTH, FOX_WIDTH, FOX_HEADS, RET_QK_WIDTH, RET_QK_WIDTH, RET_V_WIDTH, RET_V_WIDTH, N_BRANCHES * D_MODEL)
N_IN = sum(IN_SPLIT_SIZES)

kernel_name = 'hybrid_fox_retention_peer_block'


def _rms(x, g):
    xf = x.astype(jnp.float32)
    y = xf * lax.rsqrt(jnp.mean(xf * xf, axis=-1, keepdims=True) + EPS)
    return (y * g.astype(jnp.float32)).astype(x.dtype)


def _split_heads(t, n_heads):
    b, s, _ = t.shape
    return t.reshape(b, s, n_heads, -1).transpose(0, 2, 1, 3)


def _merge_heads(t):
    b, h, s, d = t.shape
    return t.transpose(0, 2, 1, 3).reshape(b, s, h * d)


def _rope(x):
    s, d = x.shape[-2], x.shape[-1]
    half = d // 2
    inv_freq = jnp.exp(-math.log(ROPE_BASE) * jnp.arange(half, dtype=jnp.float32) / half)
    ang = jnp.arange(s, dtype=jnp.float32)[:, None] * inv_freq[None, :]
    cos, sin = jnp.cos(ang), jnp.sin(ang)
    xf = x.astype(jnp.float32)
    x1, x2 = xf[..., :half], xf[..., half:]
    return jnp.concatenate([x1 * cos - x2 * sin, x1 * sin + x2 * cos], axis=-1).astype(x.dtype)


def _fox_attention(q, k, v, cum_logf):
    s_len, d = q.shape[2], q.shape[3]
    scale = d ** -0.5
    neg = jnp.finfo(jnp.float32).min
    outs = []
    for i in range(s_len // BLOCK_Q):
        lo, hi = i * BLOCK_Q, (i + 1) * BLOCK_Q
        qb = q[:, :, lo:hi]
        kb = k[:, :, :hi]
        vb = v[:, :, :hi]
        logits = jnp.einsum('bhqd,bhkd->bhqk', qb, kb).astype(jnp.float32) * scale
        logits = logits + cum_logf[:, :, lo:hi, None] - cum_logf[:, :, None, :hi]
        mask = jnp.arange(hi)[None, :] <= jnp.arange(lo, hi)[:, None]
        logits = jnp.where(mask, logits, neg)
        p = jax.nn.softmax(logits, axis=-1).astype(vb.dtype)
        outs.append(jnp.einsum('bhqk,bhkd->bhqd', p, vb))
    return jnp.concatenate(outs, axis=2)


def _retention(q, k, v):
    b, h, s_len, d = q.shape
    dv = v.shape[-1]
    nc = s_len // CHUNK
    log_g = jnp.log1p(-jnp.exp2(-5.0 - jnp.arange(h, dtype=jnp.float32)))
    idx = jnp.arange(CHUNK, dtype=jnp.float32)
    d_intra = jnp.exp(log_g[:, None, None] * jnp.abs(idx[:, None] - idx[None, :]))
    q_dec = jnp.exp(log_g[:, None] * (idx[None, :] + 1.0))
    k_dec = jnp.exp(log_g[:, None] * (CHUNK - 1.0 - idx[None, :]))
    chunk_dec = jnp.exp(log_g * CHUNK)
    qc = q.astype(jnp.float32).reshape(b, h, nc, CHUNK, d)
    kc = k.astype(jnp.float32).reshape(b, h, nc, CHUNK, d)
    vc = v.astype(jnp.float32).reshape(b, h, nc, CHUNK, dv)
    att = jnp.einsum('bhnid,bhnjd->bhnij', qc, kc) * d_intra[None, :, None]
    y_intra = jnp.einsum('bhnij,bhnje->bhnie', att, vc)
    kv = jnp.einsum('bhncd,bhnce->nbhde', kc * k_dec[None, :, None, :, None], vc)

    def step(state, kv_c):
        return state * chunk_dec[None, :, None, None] + kv_c, state

    _, states = lax.scan(step, jnp.zeros((b, h, d, dv), jnp.float32), kv)
    y_inter = jnp.einsum('bhnid,nbhde->bhnie', qc * q_dec[None, :, None, :, None], states)
    return (y_intra + y_inter).reshape(b, h, s_len, dv)


def _mixer_block(x, ln_g, w_in, b_forget, fox_q_g, fox_k_g, ret_norm_g, w_fox_up, w_ret_up, w_out):
    b, s_len, dm = x.shape
    h = _rms(x, ln_g)
    proj = h @ w_in
    split_pts = np.cumsum(IN_SPLIT_SIZES)[:-1].tolist()
    fq, fk, fv, ff, rq, rk, rv, rg, gate_logits = jnp.split(proj, split_pts, axis=-1)
    q = _rms(_split_heads(fq, FOX_HEADS), fox_q_g)
    k = _rms(_split_heads(fk, FOX_HEADS), fox_k_g)
    v = _split_heads(fv, FOX_HEADS)
    log_f = jax.nn.log_sigmoid(ff.astype(jnp.float32) + b_forget.astype(jnp.float32))
    cum_logf = jnp.cumsum(log_f.transpose(0, 2, 1), axis=-1)
    o_fox = _merge_heads(_fox_attention(q, k, v, cum_logf))
    rq_h = _rope(_split_heads(rq, RET_HEADS))
    rk_h = _rope(_split_heads(rk, RET_HEADS)) * (RET_QK_DIM ** -0.5)
    rv_h = _split_heads(rv, RET_HEADS)
    y = _retention(rq_h, rk_h, rv_h)
    mu = jnp.mean(y, axis=-1, keepdims=True)
    var = jnp.mean(jnp.square(y - mu), axis=-1, keepdims=True)
    y = _merge_heads((y - mu) * lax.rsqrt(var + EPS)) * ret_norm_g.astype(jnp.float32)
    o_ret = (jax.nn.silu(rg.astype(jnp.float32)) * y).astype(x.dtype)
    gates = jax.nn.sigmoid(gate_logits.astype(jnp.float32)).reshape(b, s_len, N_BRANCHES, dm)
    merged = gates[:, :, 0] * (o_fox @ w_fox_up) + gates[:, :, 1] * (o_ret @ w_ret_up)
    return x + merged.astype(x.dtype) @ w_out


def _memory_block(x, mem, ln_g, mem_g, w_mq, w_mkv, q_g, k_g, w_mo):
    b, s_len, _ = x.shape
    m_len = mem.shape[1]
    h = _rms(x, ln_g)
    m = _rms(mem, mem_g)
    q = _rms((h @ w_mq).reshape(b, s_len, MEM_HEADS, MEM_HEAD_DIM), q_g)
    kv = (m @ w_mkv).reshape(b, m_len, 2, MEM_HEADS, MEM_HEAD_DIM)
    k = _rms(kv[:, :, 0], k_g)
    v = kv[:, :, 1]
    logits = jnp.einsum('bshd,bmhd->bhsm', q, k).astype(jnp.float32) * (MEM_HEAD_DIM ** -0.5)
    p = jax.nn.softmax(logits, axis=-1).astype(v.dtype)
    o = jnp.einsum('bhsm,bmhd->bshd', p, v).reshape(b, s_len, MEM_WIDTH)
    return x + o @ w_mo


def _peer_block(x, ln_g, w_q, sub_keys, peer_u, peer_v):
    b, s_len, dm = x.shape
    h = _rms(x, ln_g)
    q = (h @ w_q).reshape(b, s_len, PEER_HEADS, 2, PEER_QUERY_DIM // 2)
    sc = jnp.einsum('bshpd,hpkd->bshpk', q, sub_keys).astype(jnp.float32)
    s_top, i_top = lax.top_k(sc, PEER_TOPK)
    cand = s_top[..., 0, :, None] + s_top[..., 1, None, :]
    cidx = i_top[..., 0, :, None] * PEER_NKEYS + i_top[..., 1, None, :]
    cand = cand.reshape(b, s_len, PEER_HEADS, PEER_TOPK * PEER_TOPK)
    cidx = cidx.reshape(b, s_len, PEER_HEADS, PEER_TOPK * PEER_TOPK)
    s_fin, pos = lax.top_k(cand, PEER_TOPK)
    e_idx = jnp.take_along_axis(cidx, pos, axis=-1)
    gate = jax.nn.softmax(s_fin, axis=-1)
    n_tok = b * s_len
    nb = n_tok // PEER_TOKEN_BLOCK
    hk = PEER_HEADS * PEER_TOPK
    hb = h.reshape(nb, PEER_TOKEN_BLOCK, dm)
    ib = e_idx.reshape(nb, PEER_TOKEN_BLOCK, hk)
    gb = gate.reshape(nb, PEER_TOKEN_BLOCK, hk).astype(h.dtype)

    def expert_block(args):
        hh, ii, gg = args
        act = jnp.einsum('tkd,td->tk', peer_u[ii], hh)
        w = gg * jax.nn.gelu(act)
        return jnp.einsum('tk,tkd->td', w, peer_v[ii])

    out = lax.map(expert_block, (hb, ib, gb)).reshape(b, s_len, dm)
    return x + out


def setup_inputs(seed: int = 0) -> dict:
    key = jax.random.key(seed)
    ks = jax.random.split(key, 24)
    f32 = jnp.float32
    L = DEPTH

    def nrm(k, shape, scale):
        return jax.random.normal(k, shape, f32) * scale

    def gain(k, n):
        return 1.0 + 0.02 * jax.random.normal(k, (L, n), f32)

    return {
        'x': nrm(ks[0], (BATCH, SEQ, D_MODEL), 1.0),
        'mem': nrm(ks[1], (BATCH, MEM_TOKENS, D_MODEL), 1.0),
        'ln_mix_g': gain(ks[2], D_MODEL),
        'w_in': nrm(ks[3], (L, D_MODEL, N_IN), D_MODEL ** -0.5),
        'b_forget': jax.random.uniform(ks[4], (L, FOX_HEADS), f32, 1.0, 6.0),
        'fox_q_norm_g': gain(ks[5], FOX_HEAD_DIM),
        'fox_k_norm_g': gain(ks[6], FOX_HEAD_DIM),
        'ret_norm_g': gain(ks[7], RET_V_WIDTH),
        'w_fox_up': nrm(ks[8], (L, FOX_WIDTH, D_MODEL), FOX_WIDTH ** -0.5),
        'w_ret_up': nrm(ks[9], (L, RET_V_WIDTH, D_MODEL), RET_V_WIDTH ** -0.5),
        'w_out': nrm(ks[10], (L, D_MODEL, D_MODEL), D_MODEL ** -0.5),
        'ln_mem_g': gain(ks[11], D_MODEL),
        'mem_norm_g': gain(ks[12], D_MODEL),
        'w_mq': nrm(ks[13], (L, D_MODEL, MEM_WIDTH), D_MODEL ** -0.5),
        'w_mkv': nrm(ks[14], (L, D_MODEL, 2 * MEM_WIDTH), D_MODEL ** -0.5),
        'mem_q_norm_g': gain(ks[15], MEM_HEAD_DIM),
        'mem_k_norm_g': gain(ks[16], MEM_HEAD_DIM),
        'w_mo': nrm(ks[17], (L, MEM_WIDTH, D_MODEL), MEM_WIDTH ** -0.5),
        'ln_ffn_g': gain(ks[18], D_MODEL),
        'w_peer_q': nrm(ks[19], (L, D_MODEL, PEER_HEADS * PEER_QUERY_DIM), D_MODEL ** -0.5),
        'peer_sub_keys': nrm(ks[20], (L, PEER_HEADS, 2, PEER_NKEYS, PEER_QUERY_DIM // 2), (PEER_QUERY_DIM // 2) ** -0.5),
        'peer_u': nrm(ks[21], (L, N_EXPERTS, D_MODEL), D_MODEL ** -0.5),
        'peer_v': nrm(ks[22], (L, N_EXPERTS, D_MODEL), 0.5),
    }


def reference(x, mem, ln_mix_g, w_in, b_forget, fox_q_norm_g, fox_k_norm_g, ret_norm_g, w_fox_up, w_ret_up, w_out, ln_mem_g, mem_norm_g, w_mq, w_mkv, mem_q_norm_g, mem_k_norm_g, w_mo, ln_ffn_g, w_peer_q, peer_sub_keys, peer_u, peer_v):
    for l in range(DEPTH):
        x = _mixer_block(x, ln_mix_g[l], w_in[l], b_forget[l], fox_q_norm_g[l], fox_k_norm_g[l], ret_norm_g[l], w_fox_up[l], w_ret_up[l], w_out[l])
        x = _memory_block(x, mem, ln_mem_g[l], mem_norm_g[l], w_mq[l], w_mkv[l], mem_q_norm_g[l], mem_k_norm_g[l], w_mo[l])
        x = _peer_block(x, ln_ffn_g[l], w_peer_q[l], peer_sub_keys[l], peer_u[l], peer_v[l])
    return x
```

```python
import functools
import math

import jax
import jax.numpy as jnp
from jax import lax
from jax.experimental import pallas as pl
from jax.experimental.pallas import tpu as pltpu

F32 = jnp.float32
BF16 = jnp.bfloat16

EPS = 1e-6
ROPE_BASE = 10000.0
HEAD_DIM = 128
FOX_HEADS = 8
RET_HEADS = 8
RET_CHUNK = 64
MEM_HEADS = 4
PEER_HEADS = 8
PEER_NKEYS = 128
PEER_TOPK = 16
LANES = 128
SUBLANES = 8
VMEM_LIMIT_BYTES = 56 * 1024 * 1024
NEG_BIG = -0.7 * float(jnp.finfo(jnp.float32).max)

NT_DIMS = (((1,), (1,)), ((), ()))


def _cparams(semantics):
    return pltpu.CompilerParams(dimension_semantics=semantics,
                                vmem_limit_bytes=VMEM_LIMIT_BYTES)


def _rms_rows(x, g):
    return x * lax.rsqrt(jnp.mean(x * x, axis=-1, keepdims=True) + EPS) * g


def _rms_matmul_kernel(*refs, has_aux, emit_h):
    it = iter(refs)
    x_ref, g_ref, w_ref = next(it), next(it), next(it)
    waux_ref = next(it) if has_aux else None
    o_ref = next(it)
    aux_ref = next(it) if has_aux else None
    hout_ref = next(it) if emit_h else None
    h_scr = next(it)

    @pl.when(pl.program_id(1) == 0)
    def _():
        h = _rms_rows(x_ref[...], g_ref[...]).astype(BF16)
        h_scr[...] = h
        if emit_h:
            hout_ref[...] = h
        if has_aux:
            aux_ref[...] = jnp.dot(h, waux_ref[...], preferred_element_type=F32)

    o_ref[...] = jnp.dot(h_scr[...], w_ref[...],
                         preferred_element_type=F32).astype(o_ref.dtype)


def _rms_matmul(x, g, w, *, bm, bn, out_dtype, w_aux=None, emit_h=False):
    m, k = x.shape
    n = w.shape[1]
    bm, bn = min(bm, m), min(bn, n)
    has_aux = w_aux is not None
    in_specs = [
        pl.BlockSpec((bm, k), lambda i, j: (i, 0)),
        pl.BlockSpec((1, k), lambda i, j: (0, 0)),
        pl.BlockSpec((k, bn), lambda i, j: (0, j)),
    ]
    args = [x, g.reshape(1, k), w]
    out_shape = [jax.ShapeDtypeStruct((m, n), out_dtype)]
    out_specs = [pl.BlockSpec((bm, bn), lambda i, j: (i, j))]
    if has_aux:
        na = w_aux.shape[1]
        in_specs.append(pl.BlockSpec((k, na), lambda i, j: (0, 0)))
        args.append(w_aux)
        out_shape.append(jax.ShapeDtypeStruct((m, na), F32))
        out_specs.append(pl.BlockSpec((bm, na), lambda i, j: (i, 0)))
    if emit_h:
        out_shape.append(jax.ShapeDtypeStruct((m, k), BF16))
        out_specs.append(pl.BlockSpec((bm, k), lambda i, j: (i, 0)))
    return pl.pallas_call(
        functools.partial(_rms_matmul_kernel, has_aux=has_aux, emit_h=emit_h),
        grid=(m // bm, n // bn),
        in_specs=in_specs,
        out_specs=out_specs,
        out_shape=out_shape,
        scratch_shapes=[pltpu.VMEM((bm, k), BF16)],
        compiler_params=_cparams(("parallel", "arbitrary")),
        name="rms_matmul",
    )(*args)


def _matmul_res_kernel(a_ref, b_ref, r_ref, o_ref):
    o_ref[...] = r_ref[...] + jnp.dot(a_ref[...], b_ref[...],
                                      preferred_element_type=F32)


def _matmul_res(a, b, res, *, bm, bn):
    m, k = a.shape
    n = b.shape[1]
    bm, bn = min(bm, m), min(bn, n)
    return pl.pallas_call(
        _matmul_res_kernel,
        grid=(m // bm, n // bn),
        in_specs=[pl.BlockSpec((bm, k), lambda i, j: (i, 0)),
                  pl.BlockSpec((k, bn), lambda i, j: (0, j)),
                  pl.BlockSpec((bm, bn), lambda i, j: (i, j))],
        out_specs=pl.BlockSpec((bm, bn), lambda i, j: (i, j)),
        out_shape=jax.ShapeDtypeStruct((m, n), F32),
        compiler_params=_cparams(("parallel", "arbitrary")),
        name="matmul_res",
    )(a, b, res)


def _merge_kernel(of_ref, or_ref, g0_ref, g1_ref, wf_ref, wr_ref, o_ref):
    t0 = jnp.dot(of_ref[...], wf_ref[...], preferred_element_type=F32)
    t1 = jnp.dot(or_ref[...], wr_ref[...], preferred_element_type=F32)
    g0 = jax.nn.sigmoid(g0_ref[...].astype(F32))
    g1 = jax.nn.sigmoid(g1_ref[...].astype(F32))
    o_ref[...] = (g0 * t0 + g1 * t1).astype(o_ref.dtype)


def _gated_merge(o_fox, o_ret, proj, gate_col0, w_fox_up, w_ret_up, *, bm, bn):
    m, kf = o_fox.shape
    kr = o_ret.shape[1]
    d = w_fox_up.shape[1]
    bm, bn = min(bm, m), min(bn, d)
    g0_blk = gate_col0 // bn
    g1_blk = (gate_col0 + d) // bn
    return pl.pallas_call(
        _merge_kernel,
        grid=(m // bm, d // bn),
        in_specs=[pl.BlockSpec((bm, kf), lambda i, j: (i, 0)),
                  pl.BlockSpec((bm, kr), lambda i, j: (i, 0)),
                  pl.BlockSpec((bm, bn), lambda i, j: (i, g0_blk + j)),
                  pl.BlockSpec((bm, bn), lambda i, j: (i, g1_blk + j)),
                  pl.BlockSpec((kf, bn), lambda i, j: (0, j)),
                  pl.BlockSpec((kr, bn), lambda i, j: (0, j))],
        out_specs=pl.BlockSpec((bm, bn), lambda i, j: (i, j)),
        out_shape=jax.ShapeDtypeStruct((m, d), BF16),
        compiler_params=_cparams(("parallel", "arbitrary")),
        name="gated_merge",
    )(o_fox, o_ret, proj, proj, w_fox_up, w_ret_up)


def _forget_kernel(ff_ref, b_ref, tri_ref, o_ref, carry_ref):
    @pl.when(pl.program_id(1) == 0)
    def _():
        carry_ref[...] = jnp.zeros_like(carry_ref)

    z = ff_ref[...] + b_ref[...]
    lf = jnp.minimum(z, 0.0) - jnp.log1p(jnp.exp(-jnp.abs(z)))
    hi = lf.astype(BF16)
    r1 = lf - hi.astype(F32)
    mid = r1.astype(BF16)
    lo = (r1 - mid.astype(F32)).astype(BF16)
    tri = tri_ref[...]
    c = (jnp.dot(tri, hi, preferred_element_type=F32)
         + jnp.dot(tri, mid, preferred_element_type=F32)
         + jnp.dot(tri, lo, preferred_element_type=F32))
    out = c + carry_ref[...]
    o_ref[...] = out
    carry_ref[...] = out[-1:, :]


def _forget_cumsum(ff, b_pad, batch, seq, *, tc):
    tc = min(tc, seq)
    nt = seq // tc
    tri = jnp.tril(jnp.ones((tc, tc), F32)).astype(BF16)
    return pl.pallas_call(
        _forget_kernel,
        grid=(batch, nt),
        in_specs=[pl.BlockSpec((tc, LANES), lambda b, t: (b * nt + t, 0)),
                  pl.BlockSpec((1, LANES), lambda b, t: (0, 0)),
                  pl.BlockSpec((tc, tc), lambda b, t: (0, 0))],
        out_specs=pl.BlockSpec((tc, LANES), lambda b, t: (b * nt + t, 0)),
        out_shape=jax.ShapeDtypeStruct(ff.shape, F32),
        scratch_shapes=[pltpu.VMEM((1, LANES), F32)],
        compiler_params=_cparams(("parallel", "arbitrary")),
        name="forget_cumsum",
    )(ff, b_pad, tri)


def _fox_kernel(qi_tab, ki_tab, q_ref, k_ref, v_ref, fc_ref, fr_ref, gq_ref, gk_ref,
                o_ref, qn_scr, ft_scr, m_scr, l_scr, acc_scr):
    h = pl.program_id(1)
    p = pl.program_id(2)
    qi = qi_tab[p]
    ki = ki_tab[p]
    bq, bk = q_ref.shape[0], k_ref.shape[0]

    @pl.when(ki == 0)
    def _():
        q = _rms_rows(q_ref[...].astype(F32), gq_ref[...]) * (HEAD_DIM ** -0.5)
        qn_scr[...] = q.astype(BF16)
        lane = lax.broadcasted_iota(jnp.int32, fc_ref.shape, 1)
        ft_scr[...] = jnp.sum(jnp.where(lane == h, fc_ref[...], 0.0), axis=-1, keepdims=True)
        m_scr[...] = jnp.full_like(m_scr, NEG_BIG)
        l_scr[...] = jnp.zeros_like(l_scr)
        acc_scr[...] = jnp.zeros_like(acc_scr)

    kn = _rms_rows(k_ref[...].astype(F32), gk_ref[...]).astype(BF16)
    s = lax.dot_general(qn_scr[...], kn, NT_DIMS, preferred_element_type=F32)
    s = s + ft_scr[...] - fr_ref[...]
    row = lax.broadcasted_iota(jnp.int32, (bq, bk), 0) + qi * bq
    col = lax.broadcasted_iota(jnp.int32, (bq, bk), 1) + ki * bk
    s = jnp.where(col <= row, s, NEG_BIG)
    m_prev = m_scr[...]
    m_new = jnp.maximum(m_prev, jnp.max(s, axis=-1, keepdims=True))
    alpha = jnp.exp(m_prev - m_new)
    pr = jnp.exp(s - m_new)
    l_scr[...] = alpha * l_scr[...] + jnp.sum(pr, axis=-1, keepdims=True)
    acc_scr[...] = alpha * acc_scr[...] + jnp.dot(pr.astype(BF16), v_ref[...],
                                                  preferred_element_type=F32)
    m_scr[...] = m_new

    @pl.when(ki == qi)
    def _():
        o_ref[...] = (acc_scr[...] / l_scr[...]).astype(o_ref.dtype)


def _fox_attention(proj, f_col, f_row, gq, gk, batch, seq, *, q_blk0, k_blk0, v_blk0, bq):
    bq = min(bq, seq)
    nq = seq // bq
    pairs = [(qi, ki) for qi in range(nq) for ki in range(qi + 1)]
    qi_tab = jnp.asarray([p[0] for p in pairs], jnp.int32)
    ki_tab = jnp.asarray([p[1] for p in pairs], jnp.int32)
    ntok = proj.shape[0]

    grid_spec = pltpu.PrefetchScalarGridSpec(
        num_scalar_prefetch=2,
        grid=(batch, FOX_HEADS, len(pairs)),
        in_specs=[
            pl.BlockSpec((bq, HEAD_DIM), lambda b, h, p, qt, kt: (b * nq + qt[p], q_blk0 + h)),
            pl.BlockSpec((bq, HEAD_DIM), lambda b, h, p, qt, kt: (b * nq + kt[p], k_blk0 + h)),
            pl.BlockSpec((bq, HEAD_DIM), lambda b, h, p, qt, kt: (b * nq + kt[p], v_blk0 + h)),
            pl.BlockSpec((bq, LANES), lambda b, h, p, qt, kt: (b * nq + qt[p], 0)),
            pl.BlockSpec((None, None, 1, bq), lambda b, h, p, qt, kt: (b, h, 0, kt[p])),
            pl.BlockSpec((1, HEAD_DIM), lambda b, h, p, qt, kt: (0, 0)),
            pl.BlockSpec((1, HEAD_DIM), lambda b, h, p, qt, kt: (0, 0)),
        ],
        out_specs=pl.BlockSpec((bq, HEAD_DIM), lambda b, h, p, qt, kt: (b * nq + qt[p], h)),
        scratch_shapes=[pltpu.VMEM((bq, HEAD_DIM), BF16),
                        pltpu.VMEM((bq, 1), F32),
                        pltpu.VMEM((bq, 1), F32),
                        pltpu.VMEM((bq, 1), F32),
                        pltpu.VMEM((bq, HEAD_DIM), F32)],
    )
    return pl.pallas_call(
        _fox_kernel,
        grid_spec=grid_spec,
        out_shape=jax.ShapeDtypeStruct((ntok, FOX_HEADS * HEAD_DIM), BF16),
        compiler_params=_cparams(("parallel", "parallel", "arbitrary")),
        name="fox_attention",
    )(qi_tab, ki_tab, proj, proj, proj, f_col, f_row, gq, gk)


def _retention_kernel(q_ref, k_ref, v_ref, g_ref, cos_ref, sin_ref, dmask_ref, qdec_ref,
                      kdec_ref, sdec_ref, ng_ref, o_ref, state_scr):
    @pl.when(pl.program_id(2) == 0)
    def _():
        state_scr[...] = jnp.zeros_like(state_scr)

    cosf, sinf = cos_ref[...], sin_ref[...]

    def rope(x):
        return x * cosf + pltpu.roll(x, HEAD_DIM // 2, 1) * sinf

    q = rope(q_ref[...].astype(F32))
    k = rope(k_ref[...].astype(F32)) * (HEAD_DIM ** -0.5)
    v = v_ref[...]
    qb = q.astype(BF16)
    att = lax.dot_general(qb, k.astype(BF16), NT_DIMS, preferred_element_type=F32)
    att = att * dmask_ref[...]
    y = jnp.dot(att.astype(BF16), v, preferred_element_type=F32)
    state = state_scr[...]
    y = y + qdec_ref[...] * jnp.dot(qb, state.astype(BF16), preferred_element_type=F32)
    kd_t = (k * kdec_ref[...]).T.astype(BF16)
    state_scr[...] = state * sdec_ref[...] + jnp.dot(kd_t, v, preferred_element_type=F32)

    mu = jnp.mean(y, axis=-1, keepdims=True)
    yc = y - mu
    var = jnp.mean(yc * yc, axis=-1, keepdims=True)
    yn = yc * lax.rsqrt(var + EPS) * ng_ref[...]
    gate = g_ref[...].astype(F32)
    o_ref[...] = (gate * jax.nn.sigmoid(gate) * yn).astype(o_ref.dtype)


def _retention_tables(seq, tb):
    half = HEAD_DIM // 2
    inv_freq = jnp.exp(-math.log(ROPE_BASE) * jnp.arange(half, dtype=F32) / half)
    ang = jnp.arange(seq, dtype=F32)[:, None] * inv_freq[None, :]
    cos, sin = jnp.cos(ang), jnp.sin(ang)
    cosf = jnp.concatenate([cos, cos], axis=-1)
    sinf = jnp.concatenate([-sin, sin], axis=-1)
    log_g = jnp.log1p(-jnp.exp2(-5.0 - jnp.arange(RET_HEADS, dtype=F32)))
    idx = jnp.arange(tb, dtype=F32)
    chunk = jnp.arange(tb) // RET_CHUNK
    allowed = chunk[None, :] <= chunk[:, None]
    dist = jnp.abs(idx[:, None] - idx[None, :])
    dmask = jnp.where(allowed[None], jnp.exp(log_g[:, None, None] * dist[None]), 0.0)
    qdec = jnp.exp(log_g[:, None] * (idx[None, :] + 1.0))
    kdec = jnp.exp(log_g[:, None] * (tb - 1.0 - idx[None, :]))
    sdec = jnp.exp(log_g * tb)
    bcast = lambda t: jnp.broadcast_to(t[..., None], t.shape + (HEAD_DIM,))
    return cosf, sinf, dmask, bcast(qdec), bcast(kdec), bcast(sdec[:, None])


def _retention(proj, ret_norm_g, batch, seq, *, q_blk0, k_blk0, v_blk0, g_blk0, tb):
    tb = min(tb, seq)
    nt = seq // tb
    ntok = proj.shape[0]
    cosf, sinf, dmask, qdec, kdec, sdec = _retention_tables(seq, tb)
    tok = lambda c0: pl.BlockSpec((tb, HEAD_DIM), lambda b, h, t: (b * nt + t, c0 + h))
    per_head = lambda r, c: pl.BlockSpec((None, r, c), lambda b, h, t: (h, 0, 0))
    return pl.pallas_call(
        _retention_kernel,
        grid=(batch, RET_HEADS, nt),
        in_specs=[tok(q_blk0), tok(k_blk0), tok(v_blk0), tok(g_blk0),
                  pl.BlockSpec((tb, HEAD_DIM), lambda b, h, t: (t, 0)),
                  pl.BlockSpec((tb, HEAD_DIM), lambda b, h, t: (t, 0)),
                  per_head(tb, tb), per_head(tb, HEAD_DIM), per_head(tb, HEAD_DIM),
                  per_head(1, HEAD_DIM),
                  pl.BlockSpec((1, HEAD_DIM), lambda b, h, t: (0, h))],
        out_specs=pl.BlockSpec((tb, HEAD_DIM), lambda b, h, t: (b * nt + t, h)),
        out_shape=jax.ShapeDtypeStruct((ntok, RET_HEADS * HEAD_DIM), BF16),
        scratch_shapes=[pltpu.VMEM((HEAD_DIM, HEAD_DIM), F32)],
        compiler_params=_cparams(("parallel", "parallel", "arbitrary")),
        name="retention",
    )(proj, proj, proj, proj, cosf, sinf, dmask, qdec, kdec, sdec,
      ret_norm_g.reshape(1, -1))


def _memory_kernel(x_ref, g_ref, wq_ref, kv_ref, qg_ref, kg_ref, wo_ref, o_ref):
    x = x_ref[...]
    h = _rms_rows(x, g_ref[...]).astype(BF16)
    q = jnp.dot(h, wq_ref[...], preferred_element_type=F32)
    width = MEM_HEADS * HEAD_DIM
    outs = []
    for hd in range(MEM_HEADS):
        sl = slice(hd * HEAD_DIM, (hd + 1) * HEAD_DIM)
        qn = (_rms_rows(q[:, sl], qg_ref[...]) * (HEAD_DIM ** -0.5)).astype(BF16)
        kn = _rms_rows(kv_ref[:, sl].astype(F32), kg_ref[...]).astype(BF16)
        vh = kv_ref[:, width + hd * HEAD_DIM: width + (hd + 1) * HEAD_DIM]
        s = lax.dot_general(qn, kn, NT_DIMS, preferred_element_type=F32)
        m = jnp.max(s, axis=-1, keepdims=True)
        pr = jnp.exp(s - m)
        l = jnp.sum(pr, axis=-1, keepdims=True)
        outs.append((jnp.dot(pr.astype(BF16), vh, preferred_element_type=F32) / l).astype(BF16))
    o = jnp.concatenate(outs, axis=-1)
    o_ref[...] = x + jnp.dot(o, wo_ref[...], preferred_element_type=F32)


def _memory_block(x, ln_g, w_mq, kv, q_g, k_g, w_mo, seq, mem_tokens, *, bm):
    ntok, d = x.shape
    bm = min(bm, seq)
    per_seq = seq // bm
    width = MEM_HEADS * HEAD_DIM
    const = lambda r, c: pl.BlockSpec((r, c), lambda i: (0, 0))
    return pl.pallas_call(
        _memory_kernel,
        grid=(ntok // bm,),
        in_specs=[pl.BlockSpec((bm, d), lambda i: (i, 0)),
                  const(1, d), const(d, width),
                  pl.BlockSpec((mem_tokens, 2 * width), lambda i: (i // per_seq, 0)),
                  const(1, HEAD_DIM), const(1, HEAD_DIM), const(width, d)],
        out_specs=pl.BlockSpec((bm, d), lambda i: (i, 0)),
        out_shape=jax.ShapeDtypeStruct((ntok, d), F32),
        compiler_params=_cparams(("parallel",)),
        name="memory_block",
    )(x, ln_g.reshape(1, d), w_mq, kv, q_g.reshape(1, -1), k_g.reshape(1, -1), w_mo)


def _compare_exchange(vs, i, l, descending):
    hi, lo = jnp.maximum(vs[i], vs[l]), jnp.minimum(vs[i], vs[l])
    vs[i], vs[l] = (hi, lo) if descending else (lo, hi)


def _bitonic_merge_desc(vs):
    n = len(vs)
    j = n // 2
    while j >= 1:
        for i in range(n):
            l = i ^ j
            if l > i:
                _compare_exchange(vs, i, l, True)
        j //= 2
    return vs


def _bitonic_sort_desc(vs):
    n = len(vs)
    k = 2
    while k <= n:
        j = k // 2
        while j >= 1:
            for i in range(n):
                l = i ^ j
                if l > i:
                    _compare_exchange(vs, i, l, (i & k) == 0)
            j //= 2
        k *= 2
    return vs


def _top_of_union(a, b):
    n = len(a)
    return _bitonic_merge_desc([jnp.maximum(a[r], b[n - 1 - r]) for r in range(n)])


def _top16_rows(sc):
    vs = _bitonic_sort_desc([sc[SUBLANES * r: SUBLANES * (r + 1), :] for r in range(PEER_TOPK)])
    for shift in (4, 2, 1):
        vs = _top_of_union(vs, [pltpu.roll(v, shift, 0) for v in vs])
    return vs


_PAIR_CANDIDATES = [(a, b) for a in range(PEER_TOPK) for b in range(PEER_TOPK)
                    if (a + 1) * (b + 1) <= PEER_TOPK]


def _peer_select_kernel(q_ref, sk_ref, s1_ref, a1_ref, thr_ref, a0_ref):
    t = q_ref.shape[0]
    sub = lax.broadcasted_iota(jnp.int32, (SUBLANES, t), 0)
    zero = jnp.zeros((SUBLANES, t), F32)
    u = [zero] * PEER_TOPK
    v = [zero] * PEER_TOPK
    for h in range(PEER_HEADS):
        for half, store_ref in ((0, a0_ref), (1, s1_ref)):
            c = 2 * h + half
            sc = lax.dot_general(sk_ref[c], q_ref[:, c * HEAD_DIM:(c + 1) * HEAD_DIM],
                                 NT_DIMS, preferred_element_type=F32)
            store_ref[h] = sc
            top = _top16_rows(sc)
            if half == 0:
                u = [jnp.where(sub == h, top[r], u[r]) for r in range(PEER_TOPK)]
            else:
                v = [jnp.where(sub == h, top[r], v[r]) for r in range(PEER_TOPK)]

    cands = [u[a] + v[b] for a, b in _PAIR_CANDIDATES]
    pad = (-len(cands)) % PEER_TOPK
    cands += [jnp.full((SUBLANES, t), -jnp.inf, F32)] * pad
    groups = [_bitonic_sort_desc(cands[g:g + PEER_TOPK]) for g in range(0, len(cands), PEER_TOPK)]
    best = groups[0]
    for g in groups[1:]:
        best = _top_of_union(best, g)
    tau = best[PEER_TOPK - 1]
    z = sum(jnp.exp(b - best[0]) for b in best)
    inv_z = 1.0 / z

    for h in range(PEER_HEADS):
        row = lambda arr: arr[h:h + 1, :]
        s0 = a0_ref[h]
        s1 = s1_ref[h]
        thr = jnp.full(s0.shape, jnp.inf, F32)
        for r in range(PEER_TOPK):
            vr = row(v[r])
            thr = jnp.minimum(thr, jnp.where(s0 + vr >= row(tau), vr, jnp.inf))
        thr_ref[h] = thr
        a0_ref[h] = jnp.exp(s0 - row(u[0]))
        a1_ref[h] = jnp.exp(s1 - row(v[0])) * row(inv_z)


def _peer_select(q, sub_keys, *, tb):
    ntok = q.shape[0]
    tb = min(tb, ntok)
    blk = pl.BlockSpec((PEER_HEADS, PEER_NKEYS, tb), lambda i: (0, 0, i))
    shp = jax.ShapeDtypeStruct((PEER_HEADS, PEER_NKEYS, ntok), F32)
    return pl.pallas_call(
        _peer_select_kernel,
        grid=(ntok // tb,),
        in_specs=[pl.BlockSpec((tb, q.shape[1]), lambda i: (i, 0)),
                  pl.BlockSpec(sub_keys.shape, lambda i: (0, 0, 0))],
        out_specs=[blk, blk, blk, blk],
        out_shape=[shp, shp, shp, shp],
        compiler_params=_cparams(("parallel",)),
        name="peer_select",
    )(q, sub_keys)


def _peer_expert_kernel(h_ref, u_ref, vt_ref, s1_ref, a1_ref, thr_ref, a0_ref, x_ref,
                        o_ref, acc_scr, act_scr, wg_scr):
    e = pl.program_id(1)
    be, tb = act_scr.shape

    @pl.when(e == 0)
    def _():
        acc_scr[...] = jnp.zeros_like(acc_scr)

    act_scr[...] = lax.dot_general(u_ref[...], h_ref[...], NT_DIMS, preferred_element_type=F32)
    kps = be // PEER_NKEYS
    steps_per_group = SUBLANES // kps
    group0 = pl.multiple_of((e // steps_per_group) * SUBLANES, SUBLANES)
    pos = e % steps_per_group

    def key_row(tile, ii):
        row = tile[ii:ii + 1, :]
        for sp in range(1, steps_per_group):
            row = jnp.where(pos == sp, tile[sp * kps + ii: sp * kps + ii + 1, :], row)
        return row

    for ii in range(kps):
        rows = slice(ii * PEER_NKEYS, (ii + 1) * PEER_NKEYS)
        for c in range(tb // LANES):
            lanes = slice(c * LANES, (c + 1) * LANES)
            w = jnp.zeros((PEER_NKEYS, LANES), F32)
            for hd in range(PEER_HEADS):
                thr_i = key_row(thr_ref[hd, pl.ds(group0, SUBLANES), lanes], ii)
                a0_i = key_row(a0_ref[hd, pl.ds(group0, SUBLANES), lanes], ii)
                w = w + jnp.where(s1_ref[hd, :, lanes] >= thr_i, a1_ref[hd, :, lanes] * a0_i, 0.0)
            act = act_scr[rows, lanes]
            wg_scr[rows, lanes] = (w * jax.nn.gelu(act, approximate=True)).astype(BF16)
    acc_scr[...] += jnp.dot(vt_ref[...], wg_scr[...], preferred_element_type=F32)

    @pl.when(e == pl.num_programs(1) - 1)
    def _():
        o_ref[...] = x_ref[...] + acc_scr[...].T


def _peer_experts(h, peer_u, peer_vt, s1, a1, thr, a0, x, *, tb, be):
    ntok, d = h.shape
    n_exp = peer_u.shape[0]
    tb, be = min(tb, ntok), min(be, n_exp)
    sel = pl.BlockSpec((PEER_HEADS, PEER_NKEYS, tb), lambda i, e: (0, 0, i))
    return pl.pallas_call(
        _peer_expert_kernel,
        grid=(ntok // tb, n_exp // be),
        in_specs=[pl.BlockSpec((tb, d), lambda i, e: (i, 0)),
                  pl.BlockSpec((be, d), lambda i, e: (e, 0)),
                  pl.BlockSpec((d, be), lambda i, e: (0, e)),
                  sel, sel, sel, sel,
                  pl.BlockSpec((tb, d), lambda i, e: (i, 0))],
        out_specs=pl.BlockSpec((tb, d), lambda i, e: (i, 0)),
        out_shape=jax.ShapeDtypeStruct((ntok, d), F32),
        scratch_shapes=[pltpu.VMEM((d, tb), F32),
                        pltpu.VMEM((be, tb), F32),
                        pltpu.VMEM((be, tb), BF16)],
        compiler_params=_cparams(("parallel", "arbitrary")),
        name="peer_experts",
    )(h, peer_u, peer_vt, s1, a1, thr, a0, x)


def _layer(x2d, mem2d, batch, seq, mem_tokens, ln_mix_g, w_in, b_forget, fox_q_g, fox_k_g,
           ret_norm_g, w_fox_up, w_ret_up, w_out, ln_mem_g, mem_norm_g, w_mq, w_mkv,
           mem_q_g, mem_k_g, w_mo, ln_ffn_g, w_peer_q, sub_keys, peer_u, peer_v):
    d = x2d.shape[1]
    fox_w = FOX_HEADS * HEAD_DIM
    ret_w = RET_HEADS * HEAD_DIM
    f0 = 3 * fox_w
    w_main = jnp.concatenate([w_in[:, :f0], w_in[:, f0 + FOX_HEADS:]], axis=1).astype(BF16)
    w_forget = jnp.pad(w_in[:, f0:f0 + FOX_HEADS], ((0, 0), (0, LANES - FOX_HEADS))).astype(BF16)
    blk = lambda col: col // HEAD_DIM
    q_blk0, k_blk0, v_blk0 = blk(0), blk(fox_w), blk(2 * fox_w)
    rq_blk0, rk_blk0 = blk(3 * fox_w), blk(3 * fox_w + ret_w)
    rv_blk0, rg_blk0 = blk(3 * fox_w + 2 * ret_w), blk(3 * fox_w + 3 * ret_w)
    gate_col0 = 3 * fox_w + 4 * ret_w

    proj, ff = _rms_matmul(x2d, ln_mix_g, w_main, bm=1024, bn=1024, out_dtype=BF16,
                           w_aux=w_forget)
    b_pad = jnp.pad(b_forget, (0, LANES - FOX_HEADS)).reshape(1, LANES)
    f_col = _forget_cumsum(ff, b_pad, batch, seq, tc=256)
    f_row = f_col[:, :FOX_HEADS].reshape(batch, seq, FOX_HEADS).transpose(0, 2, 1)
    f_row = f_row.reshape(batch, FOX_HEADS, 1, seq)
    o_fox = _fox_attention(proj, f_col, f_row, fox_q_g.reshape(1, -1), fox_k_g.reshape(1, -1),
                           batch, seq, q_blk0=q_blk0, k_blk0=k_blk0, v_blk0=v_blk0, bq=512)
    o_ret = _retention(proj, ret_norm_g, batch, seq, q_blk0=rq_blk0, k_blk0=rk_blk0,
                       v_blk0=rv_blk0, g_blk0=rg_blk0, tb=256)
    merged = _gated_merge(o_fox, o_ret, proj, gate_col0, w_fox_up.astype(BF16),
                          w_ret_up.astype(BF16), bm=1024, bn=1024)
    x1 = _matmul_res(merged, w_out.astype(BF16), x2d, bm=1024, bn=1024)

    kv = _rms_matmul(mem2d, mem_norm_g, w_mkv.astype(BF16), bm=1024, bn=1024, out_dtype=BF16)[0]
    x2 = _memory_block(x1, ln_mem_g, w_mq.astype(BF16), kv, mem_q_g, mem_k_g,
                       w_mo.astype(BF16), seq, mem_tokens, bm=512)

    q, h3 = _rms_matmul(x2, ln_ffn_g, w_peer_q.astype(BF16), bm=1024, bn=1024,
                        out_dtype=BF16, emit_h=True)
    sk = sub_keys.reshape(2 * PEER_HEADS, PEER_NKEYS, HEAD_DIM).astype(BF16)
    s1, a1, thr, a0 = _peer_select(q, sk, tb=128)
    return _peer_experts(h3, peer_u.astype(BF16), peer_v.T.astype(BF16), s1, a1, thr, a0, x2,
                         tb=512, be=512)


def kernel(x, mem, ln_mix_g, w_in, b_forget, fox_q_norm_g, fox_k_norm_g, ret_norm_g, w_fox_up, w_ret_up, w_out, ln_mem_g, mem_norm_g, w_mq, w_mkv, mem_q_norm_g, mem_k_norm_g, w_mo, ln_ffn_g, w_peer_q, peer_sub_keys, peer_u, peer_v):
    batch, seq, d = x.shape
    mem_tokens = mem.shape[1]
    x2d = x.reshape(batch * seq, d)
    mem2d = mem.reshape(batch * mem_tokens, d)
    for l in range(ln_mix_g.shape[0]):
        x2d = _layer(x2d, mem2d, batch, seq, mem_tokens, ln_mix_g[l], w_in[l], b_forget[l],
                     fox_q_norm_g[l], fox_k_norm_g[l], ret_norm_g[l], w_fox_up[l], w_ret_up[l],
                     w_out[l], ln_mem_g[l], mem_norm_g[l], w_mq[l], w_mkv[l], mem_q_norm_g[l],
                     mem_k_norm_g[l], w_mo[l], ln_ffn_g[l], w_peer_q[l], peer_sub_keys[l],
                     peer_u[l], peer_v[l])
    return x2d.reshape(batch, seq, d)
```

```python
import functools
import math

import jax
import jax.numpy as jnp
from jax import lax
from jax.experimental import pallas as pl
from jax.experimental.pallas import tpu as pltpu

F32 = jnp.float32
BF16 = jnp.bfloat16

EPS = 1e-6
ROPE_BASE = 10000.0
HEAD_DIM = 128
FOX_HEADS = 8
RET_HEADS = 8
RET_CHUNK = 64
MEM_HEADS = 4
PEER_HEADS = 8
PEER_NKEYS = 128
PEER_TOPK = 16
LANES = 128
SUBLANES = 8
VMEM_LIMIT_BYTES = 56 * 1024 * 1024
NEG_BIG = -0.7 * float(jnp.finfo(jnp.float32).max)

NT_DIMS = (((1,), (1,)), ((), ()))


def _cparams(semantics, flags=None):
    return pltpu.CompilerParams(dimension_semantics=semantics,
                                vmem_limit_bytes=VMEM_LIMIT_BYTES, flags=flags)


def _rms_rows(x, g):
    return x * lax.rsqrt(jnp.mean(x * x, axis=-1, keepdims=True) + EPS) * g


def _rms_matmul_kernel(*refs, has_aux, emit_h):
    it = iter(refs)
    x_ref, g_ref, w_ref = next(it), next(it), next(it)
    waux_ref = next(it) if has_aux else None
    o_ref = next(it)
    aux_ref = next(it) if has_aux else None
    hout_ref = next(it) if emit_h else None
    h_scr = next(it)

    @pl.when(pl.program_id(1) == 0)
    def _():
        h = _rms_rows(x_ref[...], g_ref[...]).astype(BF16)
        h_scr[...] = h
        if emit_h:
            hout_ref[...] = h.astype(F32).T.astype(BF16)
        if has_aux:
            aux_ref[...] = jnp.dot(h, waux_ref[...], preferred_element_type=F32)

    o_ref[...] = jnp.dot(h_scr[...], w_ref[...],
                         preferred_element_type=F32).astype(o_ref.dtype)


def _rms_matmul(x, g, w, *, bm, bn, out_dtype, w_aux=None, emit_h=False):
    m, k = x.shape
    n = w.shape[1]
    bm, bn = min(bm, m), min(bn, n)
    has_aux = w_aux is not None
    in_specs = [
        pl.BlockSpec((bm, k), lambda i, j: (i, 0)),
        pl.BlockSpec((1, k), lambda i, j: (0, 0)),
        pl.BlockSpec((k, bn), lambda i, j: (0, j)),
    ]
    args = [x, g.reshape(1, k), w]
    out_shape = [jax.ShapeDtypeStruct((m, n), out_dtype)]
    out_specs = [pl.BlockSpec((bm, bn), lambda i, j: (i, j))]
    if has_aux:
        na = w_aux.shape[1]
        in_specs.append(pl.BlockSpec((k, na), lambda i, j: (0, 0)))
        args.append(w_aux)
        out_shape.append(jax.ShapeDtypeStruct((m, na), F32))
        out_specs.append(pl.BlockSpec((bm, na), lambda i, j: (i, 0)))
    if emit_h:
        out_shape.append(jax.ShapeDtypeStruct((k, m), BF16))
        out_specs.append(pl.BlockSpec((k, bm), lambda i, j: (0, i)))
    return pl.pallas_call(
        functools.partial(_rms_matmul_kernel, has_aux=has_aux, emit_h=emit_h),
        grid=(m // bm, n // bn),
        in_specs=in_specs,
        out_specs=out_specs,
        out_shape=out_shape,
        scratch_shapes=[pltpu.VMEM((bm, k), BF16)],
        compiler_params=_cparams(("parallel", "arbitrary")),
        name="rms_matmul",
    )(*args)


def _matmul_res_kernel(a_ref, b_ref, r_ref, o_ref):
    o_ref[...] = r_ref[...] + jnp.dot(a_ref[...], b_ref[...],
                                      preferred_element_type=F32)


def _matmul_res(a, b, res, *, bm, bn):
    m, k = a.shape
    n = b.shape[1]
    bm, bn = min(bm, m), min(bn, n)
    return pl.pallas_call(
        _matmul_res_kernel,
        grid=(m // bm, n // bn),
        in_specs=[pl.BlockSpec((bm, k), lambda i, j: (i, 0)),
                  pl.BlockSpec((k, bn), lambda i, j: (0, j)),
                  pl.BlockSpec((bm, bn), lambda i, j: (i, j))],
        out_specs=pl.BlockSpec((bm, bn), lambda i, j: (i, j)),
        out_shape=jax.ShapeDtypeStruct((m, n), F32),
        compiler_params=_cparams(("parallel", "arbitrary")),
        name="matmul_res",
    )(a, b, res)


def _merge_kernel(of_ref, or_ref, g0_ref, g1_ref, wf_ref, wr_ref, o_ref):
    t0 = jnp.dot(of_ref[...], wf_ref[...], preferred_element_type=F32)
    t1 = jnp.dot(or_ref[...], wr_ref[...], preferred_element_type=F32)
    g0 = jax.nn.sigmoid(g0_ref[...].astype(F32))
    g1 = jax.nn.sigmoid(g1_ref[...].astype(F32))
    o_ref[...] = (g0 * t0 + g1 * t1).astype(o_ref.dtype)


def _gated_merge(o_fox, o_ret, proj, gate_col0, w_fox_up, w_ret_up, *, bm, bn):
    m, kf = o_fox.shape
    kr = o_ret.shape[1]
    d = w_fox_up.shape[1]
    bm, bn = min(bm, m), min(bn, d)
    g0_blk = gate_col0 // bn
    g1_blk = (gate_col0 + d) // bn
    return pl.pallas_call(
        _merge_kernel,
        grid=(m // bm, d // bn),
        in_specs=[pl.BlockSpec((bm, kf), lambda i, j: (i, 0)),
                  pl.BlockSpec((bm, kr), lambda i, j: (i, 0)),
                  pl.BlockSpec((bm, bn), lambda i, j: (i, g0_blk + j)),
                  pl.BlockSpec((bm, bn), lambda i, j: (i, g1_blk + j)),
                  pl.BlockSpec((kf, bn), lambda i, j: (0, j)),
                  pl.BlockSpec((kr, bn), lambda i, j: (0, j))],
        out_specs=pl.BlockSpec((bm, bn), lambda i, j: (i, j)),
        out_shape=jax.ShapeDtypeStruct((m, d), BF16),
        compiler_params=_cparams(("parallel", "arbitrary")),
        name="gated_merge",
    )(o_fox, o_ret, proj, proj, w_fox_up, w_ret_up)


def _forget_kernel(ff_ref, b_ref, tri_ref, o_ref, carry_ref):
    @pl.when(pl.program_id(1) == 0)
    def _():
        carry_ref[...] = jnp.zeros_like(carry_ref)

    z = ff_ref[...] + b_ref[...]
    lf = jnp.minimum(z, 0.0) - jnp.log1p(jnp.exp(-jnp.abs(z)))
    hi = lf.astype(BF16)
    r1 = lf - hi.astype(F32)
    mid = r1.astype(BF16)
    lo = (r1 - mid.astype(F32)).astype(BF16)
    tri = tri_ref[...]
    c = (jnp.dot(tri, hi, preferred_element_type=F32)
         + jnp.dot(tri, mid, preferred_element_type=F32)
         + jnp.dot(tri, lo, preferred_element_type=F32))
    out = c + carry_ref[...]
    o_ref[...] = out
    carry_ref[...] = out[-1:, :]


def _forget_cumsum(ff, b_pad, batch, seq, *, tc):
    tc = min(tc, seq)
    nt = seq // tc
    tri = jnp.tril(jnp.ones((tc, tc), F32)).astype(BF16)
    return pl.pallas_call(
        _forget_kernel,
        grid=(batch, nt),
        in_specs=[pl.BlockSpec((tc, LANES), lambda b, t: (b * nt + t, 0)),
                  pl.BlockSpec((1, LANES), lambda b, t: (0, 0)),
                  pl.BlockSpec((tc, tc), lambda b, t: (0, 0))],
        out_specs=pl.BlockSpec((tc, LANES), lambda b, t: (b * nt + t, 0)),
        out_shape=jax.ShapeDtypeStruct(ff.shape, F32),
        scratch_shapes=[pltpu.VMEM((1, LANES), F32)],
        compiler_params=_cparams(("parallel", "arbitrary")),
        name="forget_cumsum",
    )(ff, b_pad, tri)


def _fox_kernel(q_ref, k_ref, v_ref, fc_ref, fr_ref, gq_ref, gk_ref, o_ref, *, bq):
    h = pl.program_id(1)
    seq = q_ref.shape[0]
    kn = _rms_rows(k_ref[...].astype(F32), gk_ref[...]).astype(BF16)
    lane = lax.broadcasted_iota(jnp.int32, fc_ref.shape, 1)
    f_q = jnp.sum(jnp.where(lane == h, fc_ref[...], 0.0), axis=-1, keepdims=True)
    f_k = fr_ref[...]
    row = lax.broadcasted_iota(jnp.int32, (bq, bq), 0)
    col = lax.broadcasted_iota(jnp.int32, (bq, bq), 1)
    for qi in range(seq // bq):
        lo, hi = qi * bq, (qi + 1) * bq
        q = _rms_rows(q_ref[lo:hi, :].astype(F32), gq_ref[...]) * (HEAD_DIM ** -0.5)
        s = lax.dot_general(q.astype(BF16), kn[:hi, :], NT_DIMS, preferred_element_type=F32)
        s = s + f_q[lo:hi, :] - f_k[:, :hi]
        s_diag = jnp.where(col <= row, s[:, lo:hi], NEG_BIG)
        m = jnp.max(s_diag, axis=-1, keepdims=True)
        if qi > 0:
            m = jnp.maximum(m, jnp.max(s[:, :lo], axis=-1, keepdims=True))
        p_diag = jnp.exp(s_diag - m)
        l = jnp.sum(p_diag, axis=-1, keepdims=True)
        acc = jnp.dot(p_diag.astype(BF16), v_ref[lo:hi, :], preferred_element_type=F32)
        if qi > 0:
            p_past = jnp.exp(s[:, :lo] - m)
            l = l + jnp.sum(p_past, axis=-1, keepdims=True)
            acc = acc + jnp.dot(p_past.astype(BF16), v_ref[:lo, :], preferred_element_type=F32)
        o_ref[lo:hi, :] = (acc / l).astype(o_ref.dtype)


def _fox_attention(proj, f_col, f_row, gq, gk, batch, seq, *, q_blk0, k_blk0, v_blk0, bq):
    bq = min(bq, seq)
    ntok = proj.shape[0]
    tok = lambda c0: pl.BlockSpec((seq, HEAD_DIM), lambda b, h: (b, c0 + h))
    return pl.pallas_call(
        functools.partial(_fox_kernel, bq=bq),
        grid=(batch, FOX_HEADS),
        in_specs=[tok(q_blk0), tok(k_blk0), tok(v_blk0),
                  pl.BlockSpec((seq, LANES), lambda b, h: (b, 0)),
                  pl.BlockSpec((None, None, 1, seq), lambda b, h: (b, h, 0, 0)),
                  pl.BlockSpec((1, HEAD_DIM), lambda b, h: (0, 0)),
                  pl.BlockSpec((1, HEAD_DIM), lambda b, h: (0, 0))],
        out_specs=pl.BlockSpec((seq, HEAD_DIM), lambda b, h: (b, h)),
        out_shape=jax.ShapeDtypeStruct((ntok, FOX_HEADS * HEAD_DIM), BF16),
        compiler_params=_cparams(("parallel", "parallel")),
        name="fox_attention",
    )(proj, proj, proj, f_col, f_row, gq, gk)


def _retention_kernel(q_ref, k_ref, v_ref, g_ref, cos_ref, sin_ref, dmask_ref, qdec_ref,
                      kdec_ref, sdec_ref, ng_ref, o_ref):
    tb = dmask_ref.shape[0]
    seq = q_ref.shape[0]
    state = jnp.zeros((HEAD_DIM, HEAD_DIM), F32)
    for t in range(seq // tb):
        rows = slice(t * tb, (t + 1) * tb)
        cosf, sinf = cos_ref[rows, :], sin_ref[rows, :]

        def rope(x):
            return x * cosf + pltpu.roll(x, HEAD_DIM // 2, 1) * sinf

        q = rope(q_ref[rows, :].astype(F32))
        k = rope(k_ref[rows, :].astype(F32)) * (HEAD_DIM ** -0.5)
        v = v_ref[rows, :]
        qb = q.astype(BF16)
        att = lax.dot_general(qb, k.astype(BF16), NT_DIMS, preferred_element_type=F32)
        att = att * dmask_ref[...]
        y = jnp.dot(att.astype(BF16), v, preferred_element_type=F32)
        if t > 0:
            y = y + qdec_ref[...] * jnp.dot(qb, state.astype(BF16), preferred_element_type=F32)
        kd_t = (k * kdec_ref[...]).T.astype(BF16)
        state = state * sdec_ref[...] + jnp.dot(kd_t, v, preferred_element_type=F32)

        mu = jnp.mean(y, axis=-1, keepdims=True)
        yc = y - mu
        var = jnp.mean(yc * yc, axis=-1, keepdims=True)
        yn = yc * lax.rsqrt(var + EPS) * ng_ref[...]
        gate = g_ref[rows, :].astype(F32)
        o_ref[rows, :] = (gate * jax.nn.sigmoid(gate) * yn).astype(o_ref.dtype)


def _retention_tables(seq, tb):
    half = HEAD_DIM // 2
    inv_freq = jnp.exp(-math.log(ROPE_BASE) * jnp.arange(half, dtype=F32) / half)
    ang = jnp.arange(seq, dtype=F32)[:, None] * inv_freq[None, :]
    cos, sin = jnp.cos(ang), jnp.sin(ang)
    cosf = jnp.concatenate([cos, cos], axis=-1)
    sinf = jnp.concatenate([-sin, sin], axis=-1)
    log_g = jnp.log1p(-jnp.exp2(-5.0 - jnp.arange(RET_HEADS, dtype=F32)))
    idx = jnp.arange(tb, dtype=F32)
    chunk = jnp.arange(tb) // RET_CHUNK
    allowed = chunk[None, :] <= chunk[:, None]
    dist = jnp.abs(idx[:, None] - idx[None, :])
    dmask = jnp.where(allowed[None], jnp.exp(log_g[:, None, None] * dist[None]), 0.0)
    qdec = jnp.exp(log_g[:, None] * (idx[None, :] + 1.0))
    kdec = jnp.exp(log_g[:, None] * (tb - 1.0 - idx[None, :]))
    sdec = jnp.exp(log_g * tb)
    bcast = lambda t: jnp.broadcast_to(t[..., None], t.shape + (HEAD_DIM,))
    return cosf, sinf, dmask, bcast(qdec), bcast(kdec), bcast(sdec[:, None])


def _retention(proj, ret_norm_g, batch, seq, *, q_blk0, k_blk0, v_blk0, g_blk0, tb):
    tb = min(tb, seq)
    ntok = proj.shape[0]
    cosf, sinf, dmask, qdec, kdec, sdec = _retention_tables(seq, tb)
    tok = lambda c0: pl.BlockSpec((seq, HEAD_DIM), lambda b, h: (b, c0 + h))
    per_head = lambda r, c: pl.BlockSpec((None, r, c), lambda b, h: (h, 0, 0))
    return pl.pallas_call(
        _retention_kernel,
        grid=(batch, RET_HEADS),
        in_specs=[tok(q_blk0), tok(k_blk0), tok(v_blk0), tok(g_blk0),
                  pl.BlockSpec((seq, HEAD_DIM), lambda b, h: (0, 0)),
                  pl.BlockSpec((seq, HEAD_DIM), lambda b, h: (0, 0)),
                  per_head(tb, tb), per_head(tb, HEAD_DIM), per_head(tb, HEAD_DIM),
                  per_head(1, HEAD_DIM),
                  pl.BlockSpec((1, HEAD_DIM), lambda b, h: (0, h))],
        out_specs=pl.BlockSpec((seq, HEAD_DIM), lambda b, h: (b, h)),
        out_shape=jax.ShapeDtypeStruct((ntok, RET_HEADS * HEAD_DIM), BF16),
        compiler_params=_cparams(("parallel", "parallel")),
        name="retention",
    )(proj, proj, proj, proj, cosf, sinf, dmask, qdec, kdec, sdec,
      ret_norm_g.reshape(1, -1))


def _memory_kernel(x_ref, g_ref, wq_ref, kv_ref, qg_ref, kg_ref, wo_ref, o_ref):
    x = x_ref[...]
    h = _rms_rows(x, g_ref[...]).astype(BF16)
    q = jnp.dot(h, wq_ref[...], preferred_element_type=F32)
    width = MEM_HEADS * HEAD_DIM
    outs = []
    for hd in range(MEM_HEADS):
        sl = slice(hd * HEAD_DIM, (hd + 1) * HEAD_DIM)
        qn = (_rms_rows(q[:, sl], qg_ref[...]) * (HEAD_DIM ** -0.5)).astype(BF16)
        kn = _rms_rows(kv_ref[:, sl].astype(F32), kg_ref[...]).astype(BF16)
        vh = kv_ref[:, width + hd * HEAD_DIM: width + (hd + 1) * HEAD_DIM]
        s = lax.dot_general(qn, kn, NT_DIMS, preferred_element_type=F32)
        m = jnp.max(s, axis=-1, keepdims=True)
        pr = jnp.exp(s - m)
        l = jnp.sum(pr, axis=-1, keepdims=True)
        outs.append((jnp.dot(pr.astype(BF16), vh, preferred_element_type=F32) / l).astype(BF16))
    o = jnp.concatenate(outs, axis=-1)
    o_ref[...] = x + jnp.dot(o, wo_ref[...], preferred_element_type=F32)


def _memory_block(x, ln_g, w_mq, kv, q_g, k_g, w_mo, seq, mem_tokens, *, bm):
    ntok, d = x.shape
    bm = min(bm, seq)
    per_seq = seq // bm
    width = MEM_HEADS * HEAD_DIM
    const = lambda r, c: pl.BlockSpec((r, c), lambda i: (0, 0))
    return pl.pallas_call(
        _memory_kernel,
        grid=(ntok // bm,),
        in_specs=[pl.BlockSpec((bm, d), lambda i: (i, 0)),
                  const(1, d), const(d, width),
                  pl.BlockSpec((mem_tokens, 2 * width), lambda i: (i // per_seq, 0)),
                  const(1, HEAD_DIM), const(1, HEAD_DIM), const(width, d)],
        out_specs=pl.BlockSpec((bm, d), lambda i: (i, 0)),
        out_shape=jax.ShapeDtypeStruct((ntok, d), F32),
        compiler_params=_cparams(("parallel",)),
        name="memory_block",
    )(x, ln_g.reshape(1, d), w_mq, kv, q_g.reshape(1, -1), k_g.reshape(1, -1), w_mo)


def _compare_exchange(vs, i, l, descending):
    hi, lo = jnp.maximum(vs[i], vs[l]), jnp.minimum(vs[i], vs[l])
    vs[i], vs[l] = (hi, lo) if descending else (lo, hi)


def _bitonic_merge_desc(vs):
    n = len(vs)
    j = n // 2
    while j >= 1:
        for i in range(n):
            l = i ^ j
            if l > i:
                _compare_exchange(vs, i, l, True)
        j //= 2
    return vs


def _bitonic_sort_desc(vs):
    n = len(vs)
    k = 2
    while k <= n:
        j = k // 2
        while j >= 1:
            for i in range(n):
                l = i ^ j
                if l > i:
                    _compare_exchange(vs, i, l, (i & k) == 0)
            j //= 2
        k *= 2
    return vs


def _top_of_union(a, b):
    n = len(a)
    return _bitonic_merge_desc([jnp.maximum(a[r], b[n - 1 - r]) for r in range(n)])


def _top16_rows(sc):
    vs = _bitonic_sort_desc([sc[SUBLANES * r: SUBLANES * (r + 1), :] for r in range(PEER_TOPK)])
    for shift in (4, 2, 1):
        vs = _top_of_union(vs, [pltpu.roll(v, shift, 0) for v in vs])
    return vs


_PAIR_CANDIDATES = [(a, b) for a in range(PEER_TOPK) for b in range(PEER_TOPK)
                    if (a + 1) * (b + 1) <= PEER_TOPK]


def _pack_rows(x):
    return pltpu.bitcast(x.astype(BF16), jnp.uint32)


def _peer_select_kernel(q_ref, sk_ref, rank1_ref, a1_ref, cnt0_ref, a0_ref, s1_ref):
    t = q_ref.shape[0]
    sub = lax.broadcasted_iota(jnp.int32, (SUBLANES, t), 0)
    zero = jnp.zeros((SUBLANES, t), F32)
    u = [zero] * PEER_TOPK
    v = [zero] * PEER_TOPK
    for h in range(PEER_HEADS):
        for half, store_ref in ((0, a0_ref), (1, s1_ref)):
            c = 2 * h + half
            sc = lax.dot_general(sk_ref[c], q_ref[:, c * HEAD_DIM:(c + 1) * HEAD_DIM],
                                 NT_DIMS, preferred_element_type=F32)
            store_ref[h] = sc
            top = _top16_rows(sc)
            if half == 0:
                u = [jnp.where(sub == h, top[r], u[r]) for r in range(PEER_TOPK)]
            else:
                v = [jnp.where(sub == h, top[r], v[r]) for r in range(PEER_TOPK)]

    cands = [u[a] + v[b] for a, b in _PAIR_CANDIDATES]
    pad = (-len(cands)) % PEER_TOPK
    cands += [jnp.full((SUBLANES, t), -jnp.inf, F32)] * pad
    groups = [_bitonic_sort_desc(cands[g:g + PEER_TOPK]) for g in range(0, len(cands), PEER_TOPK)]
    best = groups[0]
    for g in groups[1:]:
        best = _top_of_union(best, g)
    tau = best[PEER_TOPK - 1]
    z = sum(jnp.exp(b - best[0]) for b in best)
    inv_z = 1.0 / z

    for h in range(PEER_HEADS):
        row = lambda arr: arr[h:h + 1, :]
        s0 = a0_ref[h]
        s1 = s1_ref[h]
        rank1 = jnp.zeros(s1.shape, F32)
        cnt0 = jnp.zeros(s0.shape, F32)
        for r in range(PEER_TOPK):
            vr = row(v[r])
            rank1 = rank1 + jnp.where(vr > s1, 1.0, 0.0)
            cnt0 = cnt0 + jnp.where(s0 + vr >= row(tau), 1.0, 0.0)
        rank1_ref[h] = _pack_rows(rank1)
        cnt0_ref[h] = cnt0
        a0_ref[h] = jnp.exp(s0 - row(u[0]))
        a1_ref[h] = _pack_rows(jnp.exp(s1 - row(v[0])) * row(inv_z))


def _peer_select(q, sub_keys, *, tb):
    ntok = q.shape[0]
    tb = min(tb, ntok)
    blk = pl.BlockSpec((PEER_HEADS, PEER_NKEYS, tb), lambda i: (0, 0, i))
    shp = jax.ShapeDtypeStruct((PEER_HEADS, PEER_NKEYS, ntok), F32)
    pblk = pl.BlockSpec((PEER_HEADS, PEER_NKEYS // 2, tb), lambda i: (0, 0, i))
    pshp = jax.ShapeDtypeStruct((PEER_HEADS, PEER_NKEYS // 2, ntok), jnp.uint32)
    return pl.pallas_call(
        _peer_select_kernel,
        grid=(ntok // tb,),
        in_specs=[pl.BlockSpec((tb, q.shape[1]), lambda i: (i, 0)),
                  pl.BlockSpec(sub_keys.shape, lambda i: (0, 0, 0))],
        out_specs=[pblk, pblk, blk, blk],
        out_shape=[pshp, pshp, shp, shp],
        scratch_shapes=[pltpu.VMEM((PEER_HEADS, PEER_NKEYS, tb), F32)],
        compiler_params=_cparams(("parallel",)),
        name="peer_select",
    )(q, sub_keys)


PEER_EXPERT_BLOCK = SUBLANES * PEER_NKEYS


def _peer_expert_kernel(ht_ref, u_ref, vt_ref, rank1_ref, a1_ref, cnt0_ref, a0_ref, o_ref,
                        act_scr, wg_scr):
    e = pl.program_id(1)
    tb = ht_ref.shape[1]

    @pl.when(e == 0)
    def _():
        o_ref[...] = jnp.zeros_like(o_ref)

    for half in range(2):
        cols = slice(half * tb // 2, (half + 1) * tb // 2)
        act_scr[:, cols] = jnp.dot(u_ref[...], ht_ref[:, cols], preferred_element_type=F32)
    group = pl.multiple_of(e * SUBLANES, SUBLANES)
    zero = jnp.zeros((), BF16)

    def key_row(ref, hd, r, lanes):
        row = jnp.broadcast_to(ref[hd, pl.ds(group, SUBLANES), lanes][r:r + 1, :], (SUBLANES, LANES))
        packed = jnp.concatenate([row, row], axis=0).astype(BF16)
        return jnp.concatenate([packed] * (PEER_NKEYS // (2 * SUBLANES)), axis=0)

    for c in range(tb // LANES):
        lanes = slice(c * LANES, (c + 1) * LANES)
        for r in range(SUBLANES):
            rows = slice(r * PEER_NKEYS, (r + 1) * PEER_NKEYS)
            w = jnp.zeros((PEER_NKEYS, LANES), BF16)
            for hd in range(PEER_HEADS):
                rank1 = pltpu.bitcast(rank1_ref[hd, :, lanes], BF16)
                a1 = pltpu.bitcast(a1_ref[hd, :, lanes], BF16)
                w = w + jnp.where(rank1 < key_row(cnt0_ref, hd, r, lanes),
                                  a1 * key_row(a0_ref, hd, r, lanes), zero)
            g = jax.nn.gelu(act_scr[rows, lanes], approximate=True)
            wg_scr[rows, lanes] = w * g.astype(BF16)
    o_ref[...] += jnp.dot(vt_ref[...], wg_scr[...], preferred_element_type=F32)


def _peer_experts(ht, peer_u, peer_vt, rank1, a1, cnt0, a0, *, tb):
    d, ntok = ht.shape
    n_exp = peer_u.shape[0]
    tb = min(tb, ntok)
    be = PEER_EXPERT_BLOCK
    sel = pl.BlockSpec((PEER_HEADS, PEER_NKEYS, tb), lambda i, e: (0, 0, i))
    psel = pl.BlockSpec((PEER_HEADS, PEER_NKEYS // 2, tb), lambda i, e: (0, 0, i))
    return pl.pallas_call(
        _peer_expert_kernel,
        grid=(ntok // tb, n_exp // be),
        in_specs=[pl.BlockSpec((d, tb), lambda i, e: (0, i)),
                  pl.BlockSpec((be, d), lambda i, e: (e, 0)),
                  pl.BlockSpec((d, be), lambda i, e: (0, e)),
                  psel, psel, sel, sel],
        out_specs=pl.BlockSpec((d, tb), lambda i, e: (0, i)),
        out_shape=jax.ShapeDtypeStruct((d, ntok), F32),
        scratch_shapes=[pltpu.VMEM((be, tb), F32), pltpu.VMEM((be, tb), BF16)],
        compiler_params=_cparams(("parallel", "arbitrary")),
        name="peer_experts",
    )(ht, peer_u, peer_vt, rank1, a1, cnt0, a0)


def _add_transposed_kernel(x_ref, yt_ref, o_ref):
    o_ref[...] = x_ref[...] + yt_ref[...].T


def _add_transposed(x, yt, *, bm):
    m, d = x.shape
    bm = min(bm, m)
    return pl.pallas_call(
        _add_transposed_kernel,
        grid=(m // bm,),
        in_specs=[pl.BlockSpec((bm, d), lambda i: (i, 0)),
                  pl.BlockSpec((d, bm), lambda i: (0, i))],
        out_specs=pl.BlockSpec((bm, d), lambda i: (i, 0)),
        out_shape=jax.ShapeDtypeStruct((m, d), F32),
        compiler_params=_cparams(("parallel",)),
        name="add_transposed",
    )(x, yt)


def _layer(x2d, mem2d, batch, seq, mem_tokens, ln_mix_g, w_in, b_forget, fox_q_g, fox_k_g,
           ret_norm_g, w_fox_up, w_ret_up, w_out, ln_mem_g, mem_norm_g, w_mq, w_mkv,
           mem_q_g, mem_k_g, w_mo, ln_ffn_g, w_peer_q, sub_keys, peer_u, peer_v):
    d = x2d.shape[1]
    fox_w = FOX_HEADS * HEAD_DIM
    ret_w = RET_HEADS * HEAD_DIM
    f0 = 3 * fox_w
    w_main = jnp.concatenate([w_in[:, :f0], w_in[:, f0 + FOX_HEADS:]], axis=1).astype(BF16)
    w_forget = jnp.pad(w_in[:, f0:f0 + FOX_HEADS], ((0, 0), (0, LANES - FOX_HEADS))).astype(BF16)
    blk = lambda col: col // HEAD_DIM
    q_blk0, k_blk0, v_blk0 = blk(0), blk(fox_w), blk(2 * fox_w)
    rq_blk0, rk_blk0 = blk(3 * fox_w), blk(3 * fox_w + ret_w)
    rv_blk0, rg_blk0 = blk(3 * fox_w + 2 * ret_w), blk(3 * fox_w + 3 * ret_w)
    gate_col0 = 3 * fox_w + 4 * ret_w

    proj, ff = _rms_matmul(x2d, ln_mix_g, w_main, bm=1024, bn=1024, out_dtype=BF16,
                           w_aux=w_forget)
    b_pad = jnp.pad(b_forget, (0, LANES - FOX_HEADS)).reshape(1, LANES)
    f_col = _forget_cumsum(ff, b_pad, batch, seq, tc=256)
    f_row = f_col[:, :FOX_HEADS].reshape(batch, seq, FOX_HEADS).transpose(0, 2, 1)
    f_row = f_row.reshape(batch, FOX_HEADS, 1, seq)
    o_fox = _fox_attention(proj, f_col, f_row, fox_q_g.reshape(1, -1), fox_k_g.reshape(1, -1),
                           batch, seq, q_blk0=q_blk0, k_blk0=k_blk0, v_blk0=v_blk0, bq=512)
    o_ret = _retention(proj, ret_norm_g, batch, seq, q_blk0=rq_blk0, k_blk0=rk_blk0,
                       v_blk0=rv_blk0, g_blk0=rg_blk0, tb=256)
    merged = _gated_merge(o_fox, o_ret, proj, gate_col0, w_fox_up.astype(BF16),
                          w_ret_up.astype(BF16), bm=1024, bn=1024)
    x1 = _matmul_res(merged, w_out.astype(BF16), x2d, bm=1024, bn=1024)

    kv = _rms_matmul(mem2d, mem_norm_g, w_mkv.astype(BF16), bm=1024, bn=1024, out_dtype=BF16)[0]
    x2 = _memory_block(x1, ln_mem_g, w_mq.astype(BF16), kv, mem_q_g, mem_k_g,
                       w_mo.astype(BF16), seq, mem_tokens, bm=512)

    q, h3t = _rms_matmul(x2, ln_ffn_g, w_peer_q.astype(BF16), bm=1024, bn=1024,
                         out_dtype=BF16, emit_h=True)
    sk = sub_keys.reshape(2 * PEER_HEADS, PEER_NKEYS, HEAD_DIM).astype(BF16)
    rank1, a1, cnt0, a0 = _peer_select(q, sk, tb=128)
    mix_t = _peer_experts(h3t, peer_u.astype(BF16), peer_v.T.astype(BF16), rank1, a1, cnt0, a0,
                          tb=512)
    return _add_transposed(x2, mix_t, bm=512)


def kernel(x, mem, ln_mix_g, w_in, b_forget, fox_q_norm_g, fox_k_norm_g, ret_norm_g, w_fox_up, w_ret_up, w_out, ln_mem_g, mem_norm_g, w_mq, w_mkv, mem_q_norm_g, mem_k_norm_g, w_mo, ln_ffn_g, w_peer_q, peer_sub_keys, peer_u, peer_v):
    batch, seq, d = x.shape
    mem_tokens = mem.shape[1]
    x2d = x.reshape(batch * seq, d)
    mem2d = mem.reshape(batch * mem_tokens, d)
    for l in range(ln_mix_g.shape[0]):
        x2d = _layer(x2d, mem2d, batch, seq, mem_tokens, ln_mix_g[l], w_in[l], b_forget[l],
                     fox_q_norm_g[l], fox_k_norm_g[l], ret_norm_g[l], w_fox_up[l], w_ret_up[l],
                     w_out[l], ln_mem_g[l], mem_norm_g[l], w_mq[l], w_mkv[l], mem_q_norm_g[l],
                     mem_k_norm_g[l], w_mo[l], ln_ffn_g[l], w_peer_q[l], peer_sub_keys[l],
                     peer_u[l], peer_v[l])
    return x2d.reshape(batch, seq, d)
```
